```python
import math
import jax, jax.numpy as jnp
from jax import lax
import numpy as np

D_MODEL = 2048
BATCH = 2
SEQ = 4096
DEPTH = 2

HEAD_DIM = 128
ROPE_DIM = HEAD_DIM // 4
ROPE_THETA = 500000.0

MOBA_HEADS = 8
MOBA_BLOCK = 256
MOBA_TOPK = 3
MOBA_Q_CHUNK = 32
MOBA_WIDTH = MOBA_HEADS * HEAD_DIM

DIFF_HEADS = 4
DIFF_V_DIM = 2 * HEAD_DIM
DIFF_Q_BLOCK = 128
DIFF_QK_WIDTH = DIFF_HEADS * 2 * HEAD_DIM
DIFF_V_WIDTH = DIFF_HEADS * DIFF_V_DIM

IN_COLS = 3 * MOBA_WIDTH + 2 * DIFF_QK_WIDTH + DIFF_V_WIDTH + 2 * D_MODEL

PEER_HEADS = 8
PEER_N_KEYS = 128
PEER_N_EXPERTS = PEER_N_KEYS * PEER_N_KEYS
PEER_HALF = 128
PEER_KEY_DIM = 2 * PEER_HALF
PEER_TOPK = 16
PEER_TOKEN_CHUNK = 128

LN_EPS = 1e-5
RMS_EPS = 1e-5
NEG = -1e30
DEEPNORM_ALPHA = (2 * DEPTH) ** 0.25
DEEPNORM_BETA = (8 * DEPTH) ** -0.25

kernel_name = "hybrid_moba_diffattn_peer_deepnorm"


def layer_norm(x, g, b):
    xf = x.astype(jnp.float32)
    mu = jnp.mean(xf, -1, keepdims=True)
    var = jnp.mean(jnp.square(xf - mu), -1, keepdims=True)
    return ((xf - mu) * lax.rsqrt(var + LN_EPS) * g + b).astype(x.dtype)


def partial_rope(x, pos):
    half = ROPE_DIM // 2
    inv_freq = 1.0 / (ROPE_THETA ** (jnp.arange(half, dtype=jnp.float32) / half))
    ang = pos.astype(jnp.float32)[:, None] * inv_freq[None, :]
    cos = jnp.cos(ang).astype(x.dtype)
    sin = jnp.sin(ang).astype(x.dtype)
    x1, x2, rest = x[..., :half], x[..., half:ROPE_DIM], x[..., ROPE_DIM:]
    return jnp.concatenate([x1 * cos - x2 * sin, x1 * sin + x2 * cos, rest], -1)


def moba_attention(q, k, v):
    B, H, S, hd = q.shape
    nb = -(-S // MOBA_BLOCK)
    s_pad = nb * MOBA_BLOCK
    pad = ((0, 0), (0, 0), (0, s_pad - S), (0, 0))
    kb = jnp.pad(k, pad).reshape(B, H, nb, MOBA_BLOCK, hd)
    vb = jnp.pad(v, pad).reshape(B, H, nb, MOBA_BLOCK, hd)
    k_mean = jnp.mean(kb.astype(jnp.float32), axis=3)
    n_sel = max(1, min(MOBA_TOPK, nb - 1))
    n_chunks = S // MOBA_Q_CHUNK
    qc = q.reshape(B, H, n_chunks, MOBA_Q_CHUNK, hd).transpose(2, 0, 1, 3, 4)
    scale = hd ** -0.5
    b_idx = jnp.arange(B)[:, None, None, None]
    h_idx = jnp.arange(H)[None, :, None, None]
    blk_ids = jnp.arange(nb)

    def chunk_fn(args):
        c, q_c = args
        q_pos = c * MOBA_Q_CHUNK + jnp.arange(MOBA_Q_CHUNK)
        own = (c * MOBA_Q_CHUNK) // MOBA_BLOCK
        gate = jnp.einsum('bhqd,bhnd->bhqn', q_c.astype(jnp.float32), k_mean)
        gate = jnp.where(blk_ids < own, gate, NEG)
        _, sel = lax.top_k(gate, n_sel)
        sel_valid = sel < own
        k_sel = kb[b_idx, h_idx, sel]
        v_sel = vb[b_idx, h_idx, sel]
        s_sel = jnp.einsum('bhqd,bhqnkd->bhqnk', q_c, k_sel).astype(jnp.float32) * scale
        s_sel = jnp.where(sel_valid[..., None], s_sel, NEG)
        k_own = lax.dynamic_index_in_dim(kb, own, axis=2, keepdims=False)
        v_own = lax.dynamic_index_in_dim(vb, own, axis=2, keepdims=False)
        s_own = jnp.einsum('bhqd,bhkd->bhqk', q_c, k_own).astype(jnp.float32) * scale
        k_pos = own * MOBA_BLOCK + jnp.arange(MOBA_BLOCK)
        s_own = jnp.where(k_pos[None, :] <= q_pos[:, None], s_own, NEG)
        s_all = jnp.concatenate([s_sel.reshape(B, H, MOBA_Q_CHUNK, n_sel * MOBA_BLOCK), s_own], -1)
        p = jax.nn.softmax(s_all, axis=-1).astype(v.dtype)
        p_sel = p[..., :n_sel * MOBA_BLOCK].reshape(B, H, MOBA_Q_CHUNK, n_sel, MOBA_BLOCK)
        p_own = p[..., n_sel * MOBA_BLOCK:]
        return (jnp.einsum('bhqnk,bhqnkd->bhqd', p_sel, v_sel)
                + jnp.einsum('bhqk,bhkd->bhqd', p_own, v_own))

    out = lax.map(chunk_fn, (jnp.arange(n_chunks), qc))
    return out.transpose(1, 2, 0, 3, 4).reshape(B, H, S, hd)


def diff_attention(q, k, v, lam, subln_g, lambda_init):
    B, H, _, S, hd = q.shape
    n_blocks = S // DIFF_Q_BLOCK
    qb = q.reshape(B, H, 2, n_blocks, DIFF_Q_BLOCK, hd).transpose(3, 0, 1, 2, 4, 5)
    k_pos = jnp.arange(S)
    scale = hd ** -0.5

    def block_fn(args):
        i, q_i = args
        q_pos = i * DIFF_Q_BLOCK + jnp.arange(DIFF_Q_BLOCK)
        s = jnp.einsum('bhcqd,bhckd->bhcqk', q_i, k).astype(jnp.float32) * scale
        s = jnp.where(k_pos[None, :] <= q_pos[:, None], s, NEG)
        p = jax.nn.softmax(s, axis=-1)
        w = p[:, :, 0] - lam * p[:, :, 1]
        return jnp.einsum('bhqk,bhkd->bhqd', w.astype(v.dtype), v)

    o = lax.map(block_fn, (jnp.arange(n_blocks), qb))
    o = o.transpose(1, 2, 0, 3, 4).reshape(B, H, S, DIFF_V_DIM).astype(jnp.float32)
    o = o * lax.rsqrt(jnp.mean(jnp.square(o), -1, keepdims=True) + RMS_EPS) * subln_g
    return (o * (1.0 - lambda_init)).astype(v.dtype)


def hybrid_mixer(x, w_in, w_br_moba, w_br_diff, w_out, diff_lambda, diff_subln_g, layer_idx):
    B, S, _ = x.shape
    pos = jnp.arange(S)
    h = x @ w_in
    widths = [MOBA_WIDTH, MOBA_WIDTH, MOBA_WIDTH, DIFF_QK_WIDTH, DIFF_QK_WIDTH, DIFF_V_WIDTH, D_MODEL]
    cuts = list(np.cumsum(widths))
    q_m, k_m, v_m, q_d, k_d, v_d, g_m, g_d = jnp.split(h, [int(c) for c in cuts], axis=-1)

    def mheads(t):
        return t.reshape(B, S, MOBA_HEADS, HEAD_DIM).transpose(0, 2, 1, 3)
    y_m = moba_attention(partial_rope(mheads(q_m), pos), partial_rope(mheads(k_m), pos), mheads(v_m))
    y_m = y_m.transpose(0, 2, 1, 3).reshape(B, S, MOBA_WIDTH)

    def dheads(t):
        return t.reshape(B, S, DIFF_HEADS, 2, HEAD_DIM).transpose(0, 2, 3, 1, 4)
    lamf = diff_lambda.astype(jnp.float32)
    lambda_init = 0.8 - 0.6 * math.exp(-0.3 * layer_idx)
    lam = jnp.exp(jnp.sum(lamf[0] * lamf[1])) - jnp.exp(jnp.sum(lamf[2] * lamf[3])) + lambda_init
    v_dh = v_d.reshape(B, S, DIFF_HEADS, DIFF_V_DIM).transpose(0, 2, 1, 3)
    y_d = diff_attention(partial_rope(dheads(q_d), pos), partial_rope(dheads(k_d), pos), v_dh,
                         lam, diff_subln_g, lambda_init)
    y_d = y_d.transpose(0, 2, 1, 3).reshape(B, S, DIFF_V_WIDTH)

    merged = jax.nn.sigmoid(g_m) * (y_m @ w_br_moba) + jax.nn.sigmoid(g_d) * (y_d @ w_br_diff)
    return merged @ w_out


def peer_ffn(x, w_q, sub_keys, u, v):
    B, S, D = x.shape
    T = B * S
    xt = x.reshape(T, D)
    q = (xt @ w_q).reshape(T, PEER_HEADS, 2, PEER_HALF)
    s = jnp.einsum('thcd,hcnd->thcn', q, sub_keys).astype(jnp.float32)
    s_top, i_top = lax.top_k(s, PEER_TOPK)
    cand_s = s_top[:, :, 0, :, None] + s_top[:, :, 1, None, :]
    cand_i = i_top[:, :, 0, :, None] * PEER_N_KEYS + i_top[:, :, 1, None, :]
    best_s, best_pos = lax.top_k(cand_s.reshape(T, PEER_HEADS, PEER_TOPK * PEER_TOPK), PEER_TOPK)
    expert_idx = jnp.take_along_axis(cand_i.reshape(T, PEER_HEADS, PEER_TOPK * PEER_TOPK), best_pos, -1)
    gates = jax.nn.softmax(best_s, axis=-1).astype(x.dtype)
    n_chunks = T // PEER_TOKEN_CHUNK
    hk = PEER_HEADS * PEER_TOPK

    def chunk_fn(args):
        x_c, idx_c, g_c = args
        u_c = u[idx_c]
        v_c = v[idx_c]
        act = jax.nn.gelu(jnp.einsum('cd,ced->ce', x_c, u_c), approximate=False)
        return jnp.einsum('ce,ced->cd', g_c * act, v_c)

    out = lax.map(chunk_fn, (xt.reshape(n_chunks, PEER_TOKEN_CHUNK, D),
                             expert_idx.reshape(n_chunks, PEER_TOKEN_CHUNK, hk),
                             gates.reshape(n_chunks, PEER_TOKEN_CHUNK, hk)))
    return out.reshape(B, S, D)


def setup_inputs(seed: int = 0) -> dict:
    key = jax.random.key(seed)
    ks = jax.random.split(key, 16)
    f32 = jnp.float32

    def nrm(k, shape, scale):
        return jax.random.normal(k, shape, f32) * scale

    return {
        "x": nrm(ks[0], (BATCH, SEQ, D_MODEL), 1.0),
        "w_in": nrm(ks[1], (DEPTH, D_MODEL, IN_COLS), D_MODEL ** -0.5),
        "w_br_moba": nrm(ks[2], (DEPTH, MOBA_WIDTH, D_MODEL), DEEPNORM_BETA * MOBA_WIDTH ** -0.5),
        "w_br_diff": nrm(ks[3], (DEPTH, DIFF_V_WIDTH, D_MODEL), DEEPNORM_BETA * DIFF_V_WIDTH ** -0.5),
        "w_out": nrm(ks[4], (DEPTH, D_MODEL, D_MODEL), DEEPNORM_BETA * D_MODEL ** -0.5),
        "diff_lambda": nrm(ks[5], (DEPTH, 4, HEAD_DIM), 0.1),
        "diff_subln_g": 1.0 + nrm(ks[6], (DEPTH, DIFF_V_DIM), 0.02),
        "ln1_g": 1.0 + nrm(ks[7], (DEPTH, D_MODEL), 0.02),
        "ln1_b": nrm(ks[8], (DEPTH, D_MODEL), 0.02),
        "peer_w_q": nrm(ks[9], (DEPTH, D_MODEL, PEER_HEADS * PEER_KEY_DIM), D_MODEL ** -0.5),
        "peer_sub_keys": nrm(ks[10], (DEPTH, PEER_HEADS, 2, PEER_N_KEYS, PEER_HALF), PEER_HALF ** -0.5),
        "peer_u": nrm(ks[11], (DEPTH, PEER_N_EXPERTS, D_MODEL), D_MODEL ** -0.5),
        "peer_v": nrm(ks[12], (DEPTH, PEER_N_EXPERTS, D_MODEL), DEEPNORM_BETA),
        "ln2_g": 1.0 + nrm(ks[13], (DEPTH, D_MODEL), 0.02),
        "ln2_b": nrm(ks[14], (DEPTH, D_MODEL), 0.02),
    }


def reference(x, w_in, w_br_moba, w_br_diff, w_out, diff_lambda, diff_subln_g, ln1_g, ln1_b,
              peer_w_q, peer_sub_keys, peer_u, peer_v, ln2_g, ln2_b):
    for l in range(DEPTH):
        mix = hybrid_mixer(x, w_in[l], w_br_moba[l], w_br_diff[l], w_out[l],
                           diff_lambda[l], diff_subln_g[l], l)
        x = layer_norm(DEEPNORM_ALPHA * x + mix, ln1_g[l], ln1_b[l])
        ffn = peer_ffn(x, peer_w_q[l], peer_sub_keys[l], peer_u[l], peer_v[l])
        x = layer_norm(DEEPNORM_ALPHA * x + ffn, ln2_g[l], ln2_b[l])
    return x
```

```python
import functools
import math

import jax
import jax.numpy as jnp
from jax import lax
from jax.experimental import pallas as pl
from jax.experimental.pallas import tpu as pltpu

F32 = jnp.float32
BF16 = jnp.bfloat16

HEAD_DIM = 128
ROPE_DIM = HEAD_DIM // 4
ROPE_THETA = 500000.0

MOBA_HEADS = 8
MOBA_BLOCK = 256
MOBA_TOPK = 3

DIFF_HEADS = 4
DIFF_V_DIM = 2 * HEAD_DIM

PEER_HEADS = 8
PEER_N_KEYS = 128
PEER_HALF = 128
PEER_TOPK = 16

LN_EPS = 1e-5
RMS_EPS = 1e-5
NEG = -1e30
BELOW_NEG = -3e38
ABOVE_ALL = 3e38

V7X_LANES = 128
V7X_SUBLANES = 8
V7X_VMEM_LIMIT = 56 * 1024 * 1024

_NT = (((1,), (1,)), ((), ()))


def _cparams(sem):
    return pltpu.CompilerParams(dimension_semantics=sem, vmem_limit_bytes=V7X_VMEM_LIMIT)


def _rope(a, c, s_up, s_dn):
    half = ROPE_DIM // 2
    return (a * c + pltpu.roll(a, half, 1) * s_up + pltpu.roll(a, V7X_LANES - half, 1) * s_dn)


def _proj_kernel(x_ref, w_ref, c_ref, su_ref, sd_ref, o_ref, *, mode):
    acc = jnp.dot(x_ref[...], w_ref[...], preferred_element_type=F32)
    if mode == "sigmoid":
        o_ref[...] = jax.nn.sigmoid(acc).astype(o_ref.dtype)
    elif mode == "plain":
        o_ref[...] = acc.astype(o_ref.dtype)
    else:
        j = pl.program_id(0)
        is_value = jnp.logical_or(j == 2, j == 5)

        @pl.when(is_value)
        def _():
            o_ref[...] = acc.astype(o_ref.dtype)

        @pl.when(jnp.logical_not(is_value))
        def _():
            c, su, sd = c_ref[...], su_ref[...], sd_ref[...]
            for h in range(acc.shape[1] // HEAD_DIM):
                sl = slice(h * HEAD_DIM, (h + 1) * HEAD_DIM)
                o_ref[:, sl] = _rope(acc[:, sl], c, su, sd).astype(o_ref.dtype)


def _project(x, w, tables, *, mode, col_off, n_cols, seq, tm, tn, out_dtype):
    t, k = x.shape
    off = col_off // tn
    pos_blocks = seq // tm
    tab_spec = pl.BlockSpec((tm, HEAD_DIM), lambda j, i: (i % pos_blocks, 0))
    return pl.pallas_call(
        functools.partial(_proj_kernel, mode=mode),
        grid=(n_cols // tn, t // tm),
        in_specs=[
            pl.BlockSpec((tm, k), lambda j, i: (i, 0)),
            pl.BlockSpec((k, tn), lambda j, i: (0, j + off)),
            tab_spec, tab_spec, tab_spec,
        ],
        out_specs=pl.BlockSpec((tm, tn), lambda j, i: (i, j)),
        out_shape=jax.ShapeDtypeStruct((t, n_cols), out_dtype),
        compiler_params=_cparams(("arbitrary", "arbitrary")),
        name=f"proj_{mode}",
    )(x, w, *tables)


def _rope_tables(seq):
    half = ROPE_DIM // 2
    inv_freq = 1.0 / (ROPE_THETA ** (jnp.arange(half, dtype=F32) / half))
    ang = jnp.arange(seq).astype(F32)[:, None] * inv_freq[None, :]
    cos, sin = jnp.cos(ang).astype(F32), jnp.sin(ang).astype(F32)
    ones = jnp.ones((seq, HEAD_DIM - ROPE_DIM), F32)
    zeros = jnp.zeros((seq, HEAD_DIM - ROPE_DIM), F32)
    zh = jnp.zeros((seq, half), F32)
    c = jnp.concatenate([cos, cos, ones], -1)
    s_up = jnp.concatenate([zh, sin, zeros], -1)
    s_dn = jnp.concatenate([-sin, zh, zeros], -1)
    return c, s_up, s_dn


def _moba_kernel(q_ref, k_ref, v_ref, o_ref, kmean_ref, *, n_sel, scale):
    blk = q_ref.shape[0]
    seq = k_ref.shape[0]
    own = pl.program_id(2)

    @pl.when(own == 0)
    def _():
        row = lax.broadcasted_iota(jnp.int32, (V7X_LANES, seq), 0)
        col = lax.broadcasted_iota(jnp.int32, (V7X_LANES, seq), 1)
        inside = jnp.logical_and(col >= row * blk, col < (row + 1) * blk)
        avg = jnp.where(inside, 1.0 / blk, 0.0).astype(BF16)
        kmean_ref[...] = jnp.dot(avg, k_ref[...], preferred_element_type=F32)

    q = q_ref[...]
    gate = lax.dot_general(q, kmean_ref[...].astype(BF16), _NT, preferred_element_type=F32)
    lane = lax.broadcasted_iota(jnp.int32, gate.shape, 1)
    lane_f = lane.astype(F32)
    past = lane < own
    g = jnp.where(past, gate, NEG)
    sel = jnp.zeros(gate.shape, F32)
    for _ in range(n_sel):
        m = jnp.max(g, axis=-1, keepdims=True)
        idx = jnp.min(jnp.where(g == m, lane_f, float(V7X_LANES)), axis=-1, keepdims=True)
        hit = lane_f == idx
        sel = jnp.where(hit, 1.0, sel)
        g = jnp.where(hit, BELOW_NEG, g)
    sel = jnp.where(past, sel, 0.0)

    def scores(start):
        kb = k_ref[pl.ds(start, blk), :]
        return lax.dot_general(q, kb, _NT, preferred_element_type=F32) * scale

    start0 = pl.multiple_of(own * blk, blk)
    r = lax.broadcasted_iota(jnp.int32, (blk, blk), 0)
    c = lax.broadcasted_iota(jnp.int32, (blk, blk), 1)
    s = jnp.where(c <= r, scores(start0), NEG)
    m0 = jnp.max(s, axis=-1, keepdims=True)
    p = jnp.exp(s - m0)
    l0 = jnp.sum(p, axis=-1, keepdims=True)
    acc0 = jnp.dot(p.astype(BF16), v_ref[pl.ds(start0, blk), :], preferred_element_type=F32)

    def body(n, carry):
        m_prev, l_prev, acc = carry
        start = pl.multiple_of(n * blk, blk)
        chosen = jnp.sum(jnp.where(lane == n, sel, 0.0), axis=-1, keepdims=True) > 0.0
        s = jnp.where(chosen, scores(start), NEG)
        m_new = jnp.maximum(m_prev, jnp.max(s, axis=-1, keepdims=True))
        alpha = jnp.exp(m_prev - m_new)
        p = jnp.exp(s - m_new)
        l_new = alpha * l_prev + jnp.sum(p, axis=-1, keepdims=True)
        acc = alpha * acc + jnp.dot(p.astype(BF16), v_ref[pl.ds(start, blk), :],
                                    preferred_element_type=F32)
        return m_new, l_new, acc

    _, l_fin, acc = lax.fori_loop(0, own, body, (m0, l0, acc0))
    o_ref[...] = (acc / l_fin).astype(o_ref.dtype)


def _moba(qkv, *, batch, seq):
    t = qkv.shape[0]
    blk = MOBA_BLOCK
    nq = seq // blk
    n_sel = max(1, min(MOBA_TOPK, nq - 1))
    h = MOBA_HEADS
    return pl.pallas_call(
        functools.partial(_moba_kernel, n_sel=n_sel, scale=HEAD_DIM ** -0.5),
        grid=(batch, h, nq),
        in_specs=[
            pl.BlockSpec((blk, HEAD_DIM), lambda b, hh, i: (b * nq + i, hh)),
            pl.BlockSpec((seq, HEAD_DIM), lambda b, hh, i: (b, h + hh)),
            pl.BlockSpec((seq, HEAD_DIM), lambda b, hh, i: (b, 2 * h + hh)),
        ],
        out_specs=pl.BlockSpec((blk, HEAD_DIM), lambda b, hh, i: (b * nq + i, hh)),
        out_shape=jax.ShapeDtypeStruct((t, h * HEAD_DIM), BF16),
        scratch_shapes=[pltpu.VMEM((V7X_LANES, HEAD_DIM), F32)],
        compiler_params=_cparams(("arbitrary", "arbitrary", "arbitrary")),
        name="moba_attention",
    )(qkv, qkv, qkv)


def _diff_kernel(lam_ref, g_ref, q_ref, k_ref, v_ref, o_ref, *, scale, lambda_init):
    tq = q_ref.shape[0]
    qi = pl.program_id(2)
    lv = lam_ref[...]
    lam = (jnp.exp(jnp.sum(lv[0:1] * lv[1:2], axis=-1, keepdims=True))
           - jnp.exp(jnp.sum(lv[2:3] * lv[3:4], axis=-1, keepdims=True)) + lambda_init)
    q1 = q_ref[:, :HEAD_DIM]
    q2 = q_ref[:, HEAD_DIM:]

    def scores(start):
        kb = k_ref[pl.ds(start, tq), :]
        s1 = lax.dot_general(q1, kb[:, :HEAD_DIM], _NT, preferred_element_type=F32) * scale
        s2 = lax.dot_general(q2, kb[:, HEAD_DIM:], _NT, preferred_element_type=F32) * scale
        return s1, s2

    start0 = pl.multiple_of(qi * tq, tq)
    r = lax.broadcasted_iota(jnp.int32, (tq, tq), 0)
    c = lax.broadcasted_iota(jnp.int32, (tq, tq), 1)
    causal = c <= r
    s1, s2 = scores(start0)
    vb = v_ref[pl.ds(start0, tq), :]

    def first(s):
        s = jnp.where(causal, s, NEG)
        m = jnp.max(s, axis=-1, keepdims=True)
        p = jnp.exp(s - m)
        return m, jnp.sum(p, axis=-1, keepdims=True), jnp.dot(p.astype(BF16), vb, preferred_element_type=F32)

    init = first(s1) + first(s2)

    def update(s, m_prev, l_prev, acc, vb):
        m_new = jnp.maximum(m_prev, jnp.max(s, axis=-1, keepdims=True))
        alpha = jnp.exp(m_prev - m_new)
        p = jnp.exp(s - m_new)
        l_new = alpha * l_prev + jnp.sum(p, axis=-1, keepdims=True)
        acc = alpha * acc + jnp.dot(p.astype(BF16), vb, preferred_element_type=F32)
        return m_new, l_new, acc

    def body(n, carry):
        start = pl.multiple_of(n * tq, tq)
        s1, s2 = scores(start)
        vb = v_ref[pl.ds(start, tq), :]
        return update(s1, *carry[:3], vb) + update(s2, *carry[3:], vb)

    _, l1, a1, _, l2, a2 = lax.fori_loop(0, qi, body, init)
    o = a1 / l1 - lam * (a2 / l2)
    o = o * lax.rsqrt(jnp.mean(jnp.square(o), axis=-1, keepdims=True) + RMS_EPS) * g_ref[...]
    o_ref[...] = (o * (1.0 - lambda_init)).astype(o_ref.dtype)


def _diff(qkv, diff_lambda, subln_g, *, batch, seq, lambda_init, tq):
    t = qkv.shape[0]
    nq = seq // tq
    h = DIFF_HEADS
    width = 2 * HEAD_DIM
    base = 3 * MOBA_HEADS * HEAD_DIM // width
    return pl.pallas_call(
        functools.partial(_diff_kernel, scale=HEAD_DIM ** -0.5, lambda_init=lambda_init),
        grid=(batch, h, nq),
        in_specs=[
            pl.BlockSpec((4, HEAD_DIM), lambda b, hh, i: (0, 0)),
            pl.BlockSpec((1, DIFF_V_DIM), lambda b, hh, i: (0, 0)),
            pl.BlockSpec((tq, width), lambda b, hh, i: (b * nq + i, base + hh)),
            pl.BlockSpec((seq, width), lambda b, hh, i: (b, base + h + hh)),
            pl.BlockSpec((seq, width), lambda b, hh, i: (b, base + 2 * h + hh)),
        ],
        out_specs=pl.BlockSpec((tq, DIFF_V_DIM), lambda b, hh, i: (b * nq + i, hh)),
        out_shape=jax.ShapeDtypeStruct((t, h * DIFF_V_DIM), BF16),
        compiler_params=_cparams(("arbitrary", "arbitrary", "arbitrary")),
        name="diff_attention",
    )(diff_lambda, subln_g.reshape(1, DIFF_V_DIM), qkv, qkv, qkv)


def _merge_kernel(ym_ref, yd_ref, wm_ref, wd_ref, gm_ref, gd_ref, o_ref):
    bm = jnp.dot(ym_ref[...], wm_ref[...], preferred_element_type=F32)
    bd = jnp.dot(yd_ref[...], wd_ref[...], preferred_element_type=F32)
    o_ref[...] = (gm_ref[...].astype(F32) * bm + gd_ref[...].astype(F32) * bd).astype(o_ref.dtype)


def _merge(y_m, y_d, w_m, w_d, gates, *, tm, tn):
    t, km = y_m.shape
    kd = y_d.shape[1]
    d = w_m.shape[1]
    nj = d // tn
    return pl.pallas_call(
        _merge_kernel,
        grid=(t // tm, nj),
        in_specs=[
            pl.BlockSpec((tm, km), lambda i, j: (i, 0)),
            pl.BlockSpec((tm, kd), lambda i, j: (i, 0)),
            pl.BlockSpec((km, tn), lambda i, j: (0, j)),
            pl.BlockSpec((kd, tn), lambda i, j: (0, j)),
            pl.BlockSpec((tm, tn), lambda i, j: (i, j)),
            pl.BlockSpec((tm, tn), lambda i, j: (i, nj + j)),
        ],
        out_specs=pl.BlockSpec((tm, tn), lambda i, j: (i, j)),
        out_shape=jax.ShapeDtypeStruct((t, d), BF16),
        compiler_params=_cparams(("arbitrary", "arbitrary")),
        name="branch_merge",
    )(y_m, y_d, w_m, w_d, gates, gates)


def _layer_norm_rows(y, g, b):
    mu = jnp.mean(y, axis=-1, keepdims=True)
    var = jnp.mean(jnp.square(y - mu), axis=-1, keepdims=True)
    return (y - mu) * lax.rsqrt(var + LN_EPS) * g + b


def _outproj_ln_kernel(m_ref, w_ref, x_ref, g_ref, b_ref, of_ref, ob_ref, *, alpha):
    mix = jnp.dot(m_ref[...], w_ref[...], preferred_element_type=F32)
    y = _layer_norm_rows(alpha * x_ref[...] + mix, g_ref[...], b_ref[...])
    of_ref[...] = y
    ob_ref[...] = y.astype(BF16)


def _outproj_ln(merged, w_out, x, g, b, *, alpha, tm):
    t, d = x.shape
    row = pl.BlockSpec((tm, d), lambda i: (i, 0))
    vec = pl.BlockSpec((1, d), lambda i: (0, 0))
    return pl.pallas_call(
        functools.partial(_outproj_ln_kernel, alpha=alpha),
        grid=(t // tm,),
        in_specs=[row, pl.BlockSpec((d, d), lambda i: (0, 0)), row, vec, vec],
        out_specs=[row, row],
        out_shape=[jax.ShapeDtypeStruct((t, d), F32), jax.ShapeDtypeStruct((t, d), BF16)],
        compiler_params=_cparams(("arbitrary",)),
        name="outproj_layernorm",
    )(merged, w_out, x, g.reshape(1, d), b.reshape(1, d))


def _resid_ln_kernel(f_ref, x_ref, g_ref, b_ref, of_ref, ob_ref, *, alpha):
    y = _layer_norm_rows(alpha * x_ref[...] + f_ref[...], g_ref[...], b_ref[...])
    of_ref[...] = y
    ob_ref[...] = y.astype(BF16)


def _resid_ln(f, x, g, b, *, alpha, tm):
    t, d = x.shape
    row = pl.BlockSpec((tm, d), lambda i: (i, 0))
    vec = pl.BlockSpec((1, d), lambda i: (0, 0))
    return pl.pallas_call(
        functools.partial(_resid_ln_kernel, alpha=alpha),
        grid=(t // tm,),
        in_specs=[row, row, vec, vec],
        out_specs=[row, row],
        out_shape=[jax.ShapeDtypeStruct((t, d), F32), jax.ShapeDtypeStruct((t, d), BF16)],
        compiler_params=_cparams(("arbitrary",)),
        name="residual_layernorm",
    )(f, x, g.reshape(1, d), b.reshape(1, d))


def _top_values(s, k):
    vals = []
    for _ in range(k):
        m = jnp.max(s, axis=0, keepdims=True)
        vals.append(m)
        s = jnp.where(s == m, BELOW_NEG, s)
    return vals


def _peer_route(q_ref, sk_ref, s1_s, e1_s, c2_s, e2_s):
    k = PEER_TOPK
    for h in range(PEER_HEADS):
        qa = q_ref[:, (2 * h) * PEER_HALF:(2 * h + 1) * PEER_HALF]
        qb = q_ref[:, (2 * h + 1) * PEER_HALF:(2 * h + 2) * PEER_HALF]
        s1 = lax.dot_general(sk_ref[h, 0], qa, _NT, preferred_element_type=F32)
        s2 = lax.dot_general(sk_ref[h, 1], qb, _NT, preferred_element_type=F32)
        v1 = _top_values(s1, k)
        v2 = _top_values(s2, k)
        cand = [v1[a] + v2[b] for a in range(k) for b in range(k) if (a + 1) * (b + 1) <= k]
        pad = (-len(cand)) % 8
        cand = jnp.concatenate(cand + [jnp.full_like(cand[0], BELOW_NEG)] * pad, axis=0)
        best = _top_values(cand, k)
        thr = best[k - 1]
        z = sum(jnp.exp(bv - best[0]) for bv in best)
        c2 = jnp.full_like(s2, ABOVE_ALL)
        for a in range(k):
            c2 = jnp.where(v1[a] + s2 >= thr, v1[a], c2)
        s1_s[h] = s1
        e1_s[h] = jnp.exp(s1 - v1[0]) / z
        c2_s[h] = c2
        e2_s[h] = jnp.exp(s2 - v2[0])


def _gelu_exact(x):
    return 0.5 * x * (1.0 + lax.erf(x * math.sqrt(0.5)))


def _peer_kernel(q_ref, sk_ref, xt_ref, u_ref, vt_ref, o_ref,
                 s1_s, e1_s, c2_s, e2_s, sc_s, h_s, acc_s):
    e = pl.program_id(1)
    te = u_ref.shape[0]
    tt = xt_ref.shape[1]
    rows_per_step = te // PEER_N_KEYS

    @pl.when(e == 0)
    def _():
        _peer_route(q_ref, sk_ref, s1_s, e1_s, c2_s, e2_s)
        acc_s[...] = jnp.zeros_like(acc_s)

    sc_s[...] = jnp.dot(u_ref[...], xt_ref[...], preferred_element_type=F32)

    i0 = pl.multiple_of(e * rows_per_step, V7X_SUBLANES)
    for tc in range(tt // V7X_LANES):
        cols = slice(tc * V7X_LANES, (tc + 1) * V7X_LANES)
        s1g = [s1_s[h, pl.ds(i0, rows_per_step), cols] for h in range(PEER_HEADS)]
        e1g = [e1_s[h, pl.ds(i0, rows_per_step), cols] for h in range(PEER_HEADS)]
        for ii in range(rows_per_step):
            rows = slice(ii * PEER_N_KEYS, (ii + 1) * PEER_N_KEYS)
            w = jnp.zeros((PEER_N_KEYS, V7X_LANES), F32)
            for h in range(PEER_HEADS):
                hit = s1g[h][ii:ii + 1, :] >= c2_s[h, :, cols]
                w = w + e1g[h][ii:ii + 1, :] * jnp.where(hit, e2_s[h, :, cols], 0.0)
            h_s[rows, cols] = (w * _gelu_exact(sc_s[rows, cols])).astype(BF16)
    acc_s[...] += jnp.dot(vt_ref[...], h_s[...], preferred_element_type=F32)

    @pl.when(e == pl.num_programs(1) - 1)
    def _():
        o_ref[...] = acc_s[...].T


def _peer(q, sub_keys, x_t, u, v_t, *, tt, te):
    t = q.shape[0]
    d, n_exp = v_t.shape
    tok = lambda shape: pltpu.VMEM(shape, F32)
    per_head = (PEER_HEADS, PEER_N_KEYS, tt)
    return pl.pallas_call(
        _peer_kernel,
        grid=(t // tt, n_exp // te),
        in_specs=[
            pl.BlockSpec((tt, q.shape[1]), lambda i, e: (i, 0)),
            pl.BlockSpec(sub_keys.shape, lambda i, e: (0, 0, 0, 0)),
            pl.BlockSpec((d, tt), lambda i, e: (0, i)),
            pl.BlockSpec((te, d), lambda i, e: (e, 0)),
            pl.BlockSpec((d, te), lambda i, e: (0, e)),
        ],
        out_specs=pl.BlockSpec((tt, d), lambda i, e: (i, 0)),
        out_shape=jax.ShapeDtypeStruct((t, d), F32),
        scratch_shapes=[tok(per_head), tok(per_head), tok(per_head), tok(per_head),
                        tok((te, tt)), pltpu.VMEM((te, tt), BF16), tok((d, tt))],
        compiler_params=_cparams(("arbitrary", "arbitrary")),
        name="peer_experts",
    )(q, sub_keys, x_t, u, v_t)


def _tile(n, want):
    return want if n % want == 0 else n


def kernel(x, w_in, w_br_moba, w_br_diff, w_out, diff_lambda, diff_subln_g, ln1_g, ln1_b,
           peer_w_q, peer_sub_keys, peer_u, peer_v, ln2_g, ln2_b):
    batch, seq, d = x.shape
    depth = w_in.shape[0]
    t = batch * seq
    alpha = (2 * depth) ** 0.25
    moba_w = MOBA_HEADS * HEAD_DIM
    qkv_cols = 3 * moba_w + 3 * DIFF_HEADS * DIFF_V_DIM
    tables = _rope_tables(seq)
    tm_proj = _tile(seq, 1024)

    xf = x.reshape(t, d)
    xb = xf.astype(BF16)
    for l in range(depth):
        lambda_init = 0.8 - 0.6 * math.exp(-0.3 * l)
        w_in_b = w_in[l].astype(BF16)
        qkv = _project(xb, w_in_b, tables, mode="rope", col_off=0, n_cols=qkv_cols, seq=seq,
                       tm=tm_proj, tn=moba_w, out_dtype=BF16)
        gates = _project(xb, w_in_b, tables, mode="sigmoid", col_off=qkv_cols, n_cols=2 * d,
                         seq=seq, tm=tm_proj, tn=min(moba_w, 2 * d), out_dtype=BF16)
        y_m = _moba(qkv, batch=batch, seq=seq)
        y_d = _diff(qkv, diff_lambda[l], diff_subln_g[l], batch=batch, seq=seq,
                    lambda_init=lambda_init, tq=_tile(seq, 256))
        merged = _merge(y_m, y_d, w_br_moba[l].astype(BF16), w_br_diff[l].astype(BF16), gates,
                        tm=_tile(t, 1024), tn=_tile(d, 1024))
        xf, xb = _outproj_ln(merged, w_out[l].astype(BF16), xf, ln1_g[l], ln1_b[l],
                             alpha=alpha, tm=_tile(t, 256))
        q = _project(xb, peer_w_q[l].astype(BF16), tables, mode="plain", col_off=0,
                     n_cols=peer_w_q.shape[2], seq=seq, tm=tm_proj, tn=_tile(peer_w_q.shape[2], 1024),
                     out_dtype=BF16)
        ffn = _peer(q, peer_sub_keys[l].astype(BF16), xb.T, peer_u[l].astype(BF16),
                    peer_v[l].T.astype(BF16), tt=_tile(t, 512), te=1024)
        xf, xb = _resid_ln(ffn, xf, ln2_g[l], ln2_b[l], alpha=alpha, tm=_tile(t, 512))
    return xf.reshape(batch, seq, d)
```

```python
import functools
import math

import jax
import jax.numpy as jnp
from jax import lax
from jax.experimental import pallas as pl
from jax.experimental.pallas import tpu as pltpu

F32 = jnp.float32
BF16 = jnp.bfloat16

HEAD_DIM = 128
ROPE_DIM = HEAD_DIM // 4
ROPE_THETA = 500000.0

MOBA_HEADS = 8
MOBA_BLOCK = 256
MOBA_TOPK = 3

DIFF_HEADS = 4
DIFF_V_DIM = 2 * HEAD_DIM

PEER_HEADS = 8
PEER_N_KEYS = 128
PEER_HALF = 128
PEER_TOPK = 16

LN_EPS = 1e-5
RMS_EPS = 1e-5
NEG = -1e30
BELOW_NEG = -3e38
ABOVE_ALL = 3e38

V7X_LANES = 128
V7X_SUBLANES = 8
V7X_VMEM_LIMIT = 56 * 1024 * 1024

_NT = (((1,), (1,)), ((), ()))


def _cparams(sem):
    return pltpu.CompilerParams(dimension_semantics=sem, vmem_limit_bytes=V7X_VMEM_LIMIT)


def _rope(a, c, s_up, s_dn):
    half = ROPE_DIM // 2
    return (a * c + pltpu.roll(a, half, 1) * s_up + pltpu.roll(a, V7X_LANES - half, 1) * s_dn)


def _proj_kernel(x_ref, w_ref, c_ref, su_ref, sd_ref, o_ref, *, mode):
    acc = jnp.dot(x_ref[...], w_ref[...], preferred_element_type=F32)
    if mode == "sigmoid":
        o_ref[...] = jax.nn.sigmoid(acc).astype(o_ref.dtype)
    elif mode == "plain":
        o_ref[...] = acc.astype(o_ref.dtype)
    else:
        j = pl.program_id(0)
        is_value = jnp.logical_or(j == 2, j == 5)

        @pl.when(is_value)
        def _():
            o_ref[...] = acc.astype(o_ref.dtype)

        @pl.when(jnp.logical_not(is_value))
        def _():
            c, su, sd = c_ref[...], su_ref[...], sd_ref[...]
            for h in range(acc.shape[1] // HEAD_DIM):
                sl = slice(h * HEAD_DIM, (h + 1) * HEAD_DIM)
                o_ref[:, sl] = _rope(acc[:, sl], c, su, sd).astype(o_ref.dtype)


def _project(x, w, tables, *, mode, col_off, n_cols, seq, tm, tn, out_dtype):
    t, k = x.shape
    off = col_off // tn
    pos_blocks = seq // tm
    tab_spec = pl.BlockSpec((tm, HEAD_DIM), lambda j, i: (i % pos_blocks, 0))
    return pl.pallas_call(
        functools.partial(_proj_kernel, mode=mode),
        grid=(n_cols // tn, t // tm),
        in_specs=[
            pl.BlockSpec((tm, k), lambda j, i: (i, 0)),
            pl.BlockSpec((k, tn), lambda j, i: (0, j + off)),
            tab_spec, tab_spec, tab_spec,
        ],
        out_specs=pl.BlockSpec((tm, tn), lambda j, i: (i, j)),
        out_shape=jax.ShapeDtypeStruct((t, n_cols), out_dtype),
        compiler_params=_cparams(("arbitrary", "arbitrary")),
        name=f"proj_{mode}",
    )(x, w, *tables)


def _rope_tables(seq):
    half = ROPE_DIM // 2
    inv_freq = 1.0 / (ROPE_THETA ** (jnp.arange(half, dtype=F32) / half))
    ang = jnp.arange(seq).astype(F32)[:, None] * inv_freq[None, :]
    cos, sin = jnp.cos(ang).astype(F32), jnp.sin(ang).astype(F32)
    ones = jnp.ones((seq, HEAD_DIM - ROPE_DIM), F32)
    zeros = jnp.zeros((seq, HEAD_DIM - ROPE_DIM), F32)
    zh = jnp.zeros((seq, half), F32)
    c = jnp.concatenate([cos, cos, ones], -1)
    s_up = jnp.concatenate([zh, sin, zeros], -1)
    s_dn = jnp.concatenate([-sin, zh, zeros], -1)
    return c, s_up, s_dn


def _lane_tile(a, width):
    return a if width == a.shape[1] else jnp.concatenate([a] * (width // a.shape[1]), axis=1)


def _softmax_block(s, vb, m_ref, l_ref, acc_ref, first):
    m_cur = jnp.max(s, axis=-1, keepdims=True)
    if first:
        m_new = jnp.broadcast_to(m_cur, m_ref.shape)
    else:
        m_prev = m_ref[...]
        m_new = jnp.maximum(m_prev, m_cur)
        alpha = jnp.exp(m_prev - m_new)
    p = jnp.exp(s - _lane_tile(m_new, s.shape[1]))
    l_cur = jnp.sum(p, axis=-1, keepdims=True)
    pv = lax.dot_general(p, vb, (((1,), (0,)), ((), ())), preferred_element_type=F32)
    if first:
        l_ref[...] = jnp.broadcast_to(l_cur, l_ref.shape)
        acc_ref[...] = pv
    else:
        l_ref[...] = alpha * l_ref[...] + l_cur
        acc_ref[...] = _lane_tile(alpha, pv.shape[1]) * acc_ref[...] + pv
    m_ref[...] = m_new


def _moba_kernel(q_ref, k_ref, v_ref, o_ref, kmean_s, sel_s, m_s, l_s, acc_s, *, n_sel, scale):
    blk = q_ref.shape[0]
    seq = k_ref.shape[0]
    heads = q_ref.shape[1] // HEAD_DIM
    own = pl.program_id(1)
    head_cols = [slice(h * HEAD_DIM, (h + 1) * HEAD_DIM) for h in range(heads)]

    @pl.when(own == 0)
    def _():
        row = lax.broadcasted_iota(jnp.int32, (V7X_LANES, seq), 0)
        col = lax.broadcasted_iota(jnp.int32, (V7X_LANES, seq), 1)
        inside = jnp.logical_and(col >= row * blk, col < (row + 1) * blk)
        avg = jnp.where(inside, 1.0 / blk, 0.0).astype(BF16)
        kmean_s[...] = jnp.dot(avg, k_ref[...], preferred_element_type=F32)

    lane = lax.broadcasted_iota(jnp.int32, (blk, V7X_LANES), 1)
    lane_f = lane.astype(F32)
    past = lane < own
    for h, hc in enumerate(head_cols):
        gate = lax.dot_general(q_ref[:, hc], kmean_s[:, hc].astype(BF16), _NT,
                               preferred_element_type=F32)
        g = jnp.where(past, gate, NEG)
        sel = jnp.zeros(gate.shape, F32)
        for _ in range(n_sel):
            m = jnp.max(g, axis=-1, keepdims=True)
            idx = jnp.min(jnp.where(g == m, lane_f, float(V7X_LANES)), axis=-1, keepdims=True)
            hit = lane_f == idx
            sel = jnp.where(hit, 1.0, sel)
            g = jnp.where(hit, BELOW_NEG, g)
        sel_s[h] = jnp.where(past, sel, 0.0)

    def scores(h, start):
        kb = k_ref[pl.ds(start, blk), head_cols[h]]
        return lax.dot_general(q_ref[:, head_cols[h]], kb, _NT, preferred_element_type=F32) * scale

    start0 = pl.multiple_of(own * blk, blk)
    r = lax.broadcasted_iota(jnp.int32, (blk, blk), 0)
    c = lax.broadcasted_iota(jnp.int32, (blk, blk), 1)
    causal = c <= r
    for h, hc in enumerate(head_cols):
        s = jnp.where(causal, scores(h, start0), NEG)
        _softmax_block(s, v_ref[pl.ds(start0, blk), hc], m_s.at[h], l_s.at[h], acc_s.at[h], True)

    def body(n, carry):
        start = pl.multiple_of(n * blk, blk)
        for h, hc in enumerate(head_cols):
            chosen = jnp.sum(jnp.where(lane == n, sel_s[h], 0.0), axis=-1, keepdims=True) > 0.0
            s = jnp.where(chosen, scores(h, start), NEG)
            _softmax_block(s, v_ref[pl.ds(start, blk), hc], m_s.at[h], l_s.at[h], acc_s.at[h], False)
        return carry

    lax.fori_loop(0, own, body, 0)
    for h, hc in enumerate(head_cols):
        o_ref[:, hc] = (acc_s[h] / l_s[h]).astype(o_ref.dtype)


def _moba(qkv, *, batch, seq):
    t = qkv.shape[0]
    blk = MOBA_BLOCK
    nq = seq // blk
    n_sel = max(1, min(MOBA_TOPK, nq - 1))
    h = MOBA_HEADS
    width = h * HEAD_DIM
    stat = pltpu.VMEM((h, blk, V7X_LANES), F32)
    return pl.pallas_call(
        functools.partial(_moba_kernel, n_sel=n_sel, scale=HEAD_DIM ** -0.5),
        grid=(batch, nq),
        in_specs=[
            pl.BlockSpec((blk, width), lambda b, i: (b * nq + i, 0)),
            pl.BlockSpec((seq, width), lambda b, i: (b, 1)),
            pl.BlockSpec((seq, width), lambda b, i: (b, 2)),
        ],
        out_specs=pl.BlockSpec((blk, width), lambda b, i: (b * nq + i, 0)),
        out_shape=jax.ShapeDtypeStruct((t, width), BF16),
        scratch_shapes=[pltpu.VMEM((V7X_LANES, width), F32), stat, stat, stat,
                        pltpu.VMEM((h, blk, HEAD_DIM), F32)],
        compiler_params=_cparams(("arbitrary", "arbitrary")),
        name="moba_attention",
    )(qkv, qkv, qkv)


def _diff_kernel(lam_ref, g_ref, q_ref, k_ref, v_ref, o_ref, m_s, l_s, acc_s, *, scale, lambda_init):
    tq = q_ref.shape[0]
    heads = q_ref.shape[1] // (2 * HEAD_DIM)
    qi = pl.program_id(1)
    lv = lam_ref[...]
    lam = (jnp.exp(jnp.sum(lv[0:1] * lv[1:2], axis=-1, keepdims=True))
           - jnp.exp(jnp.sum(lv[2:3] * lv[3:4], axis=-1, keepdims=True)) + lambda_init)

    def qk_cols(h, c):
        return slice((2 * h + c) * HEAD_DIM, (2 * h + c + 1) * HEAD_DIM)

    def v_cols(h):
        return slice(h * DIFF_V_DIM, (h + 1) * DIFF_V_DIM)

    def step(start, mask, first):
        for h in range(heads):
            vb = v_ref[pl.ds(start, tq), v_cols(h)]
            for c in range(2):
                s = lax.dot_general(q_ref[:, qk_cols(h, c)], k_ref[pl.ds(start, tq), qk_cols(h, c)],
                                    _NT, preferred_element_type=F32) * scale
                if mask is not None:
                    s = jnp.where(mask, s, NEG)
                i = 2 * h + c
                _softmax_block(s, vb, m_s.at[i], l_s.at[i], acc_s.at[i], first)

    r = lax.broadcasted_iota(jnp.int32, (tq, tq), 0)
    c = lax.broadcasted_iota(jnp.int32, (tq, tq), 1)
    step(pl.multiple_of(qi * tq, tq), c <= r, True)

    def body(n, carry):
        step(pl.multiple_of(n * tq, tq), None, False)
        return carry

    lax.fori_loop(0, qi, body, 0)
    for h in range(heads):
        a1 = acc_s[2 * h] / _lane_tile(l_s[2 * h], DIFF_V_DIM)
        a2 = acc_s[2 * h + 1] / _lane_tile(l_s[2 * h + 1], DIFF_V_DIM)
        o = a1 - lam * a2
        o = o * lax.rsqrt(jnp.mean(jnp.square(o), axis=-1, keepdims=True) + RMS_EPS) * g_ref[...]
        o_ref[:, v_cols(h)] = (o * (1.0 - lambda_init)).astype(o_ref.dtype)


def _diff(qkv, diff_lambda, subln_g, *, batch, seq, lambda_init, tq):
    t = qkv.shape[0]
    nq = seq // tq
    h = DIFF_HEADS
    width = h * DIFF_V_DIM
    base = 3 * MOBA_HEADS * HEAD_DIM // width
    stat = pltpu.VMEM((2 * h, tq, V7X_LANES), F32)
    return pl.pallas_call(
        functools.partial(_diff_kernel, scale=HEAD_DIM ** -0.5, lambda_init=lambda_init),
        grid=(batch, nq),
        in_specs=[
            pl.BlockSpec((4, HEAD_DIM), lambda b, i: (0, 0)),
            pl.BlockSpec((1, DIFF_V_DIM), lambda b, i: (0, 0)),
            pl.BlockSpec((tq, width), lambda b, i: (b * nq + i, base)),
            pl.BlockSpec((seq, width), lambda b, i: (b, base + 1)),
            pl.BlockSpec((seq, width), lambda b, i: (b, base + 2)),
        ],
        out_specs=pl.BlockSpec((tq, width), lambda b, i: (b * nq + i, 0)),
        out_shape=jax.ShapeDtypeStruct((t, width), BF16),
        scratch_shapes=[stat, stat, pltpu.VMEM((2 * h, tq, DIFF_V_DIM), F32)],
        compiler_params=_cparams(("arbitrary", "arbitrary")),
        name="diff_attention",
    )(diff_lambda, subln_g.reshape(1, DIFF_V_DIM), qkv, qkv, qkv)


def _merge_kernel(ym_ref, yd_ref, wm_ref, wd_ref, gm_ref, gd_ref, o_ref):
    bm = jnp.dot(ym_ref[...], wm_ref[...], preferred_element_type=F32)
    bd = jnp.dot(yd_ref[...], wd_ref[...], preferred_element_type=F32)
    o_ref[...] = (gm_ref[...].astype(F32) * bm + gd_ref[...].astype(F32) * bd).astype(o_ref.dtype)


def _merge(y_m, y_d, w_m, w_d, gates, *, tm, tn):
    t, km = y_m.shape
    kd = y_d.shape[1]
    d = w_m.shape[1]
    nj = d // tn
    return pl.pallas_call(
        _merge_kernel,
        grid=(t // tm, nj),
        in_specs=[
            pl.BlockSpec((tm, km), lambda i, j: (i, 0)),
            pl.BlockSpec((tm, kd), lambda i, j: (i, 0)),
            pl.BlockSpec((km, tn), lambda i, j: (0, j)),
            pl.BlockSpec((kd, tn), lambda i, j: (0, j)),
            pl.BlockSpec((tm, tn), lambda i, j: (i, j)),
            pl.BlockSpec((tm, tn), lambda i, j: (i, nj + j)),
        ],
        out_specs=pl.BlockSpec((tm, tn), lambda i, j: (i, j)),
        out_shape=jax.ShapeDtypeStruct((t, d), BF16),
        compiler_params=_cparams(("arbitrary", "arbitrary")),
        name="branch_merge",
    )(y_m, y_d, w_m, w_d, gates, gates)


def _layer_norm_rows(y, g, b):
    mu = jnp.mean(y, axis=-1, keepdims=True)
    var = jnp.mean(jnp.square(y - mu), axis=-1, keepdims=True)
    return (y - mu) * lax.rsqrt(var + LN_EPS) * g + b


def _outproj_ln_kernel(m_ref, w_ref, x_ref, g_ref, b_ref, of_ref, ob_ref, *, alpha):
    mix = jnp.dot(m_ref[...], w_ref[...], preferred_element_type=F32)
    y = _layer_norm_rows(alpha * x_ref[...] + mix, g_ref[...], b_ref[...])
    of_ref[...] = y
    ob_ref[...] = y.astype(BF16)


def _outproj_ln(merged, w_out, x, g, b, *, alpha, tm):
    t, d = x.shape
    row = pl.BlockSpec((tm, d), lambda i: (i, 0))
    vec = pl.BlockSpec((1, d), lambda i: (0, 0))
    return pl.pallas_call(
        functools.partial(_outproj_ln_kernel, alpha=alpha),
        grid=(t // tm,),
        in_specs=[row, pl.BlockSpec((d, d), lambda i: (0, 0)), row, vec, vec],
        out_specs=[row, row],
        out_shape=[jax.ShapeDtypeStruct((t, d), F32), jax.ShapeDtypeStruct((t, d), BF16)],
        compiler_params=_cparams(("arbitrary",)),
        name="outproj_layernorm",
    )(merged, w_out, x, g.reshape(1, d), b.reshape(1, d))


def _resid_ln_kernel(f_ref, x_ref, g_ref, b_ref, of_ref, ob_ref, *, alpha):
    y = _layer_norm_rows(alpha * x_ref[...] + f_ref[...], g_ref[...], b_ref[...])
    of_ref[...] = y
    ob_ref[...] = y.astype(BF16)


def _resid_ln(f, x, g, b, *, alpha, tm):
    t, d = x.shape
    row = pl.BlockSpec((tm, d), lambda i: (i, 0))
    vec = pl.BlockSpec((1, d), lambda i: (0, 0))
    return pl.pallas_call(
        functools.partial(_resid_ln_kernel, alpha=alpha),
        grid=(t // tm,),
        in_specs=[row, row, vec, vec],
        out_specs=[row, row],
        out_shape=[jax.ShapeDtypeStruct((t, d), F32), jax.ShapeDtypeStruct((t, d), BF16)],
        compiler_params=_cparams(("arbitrary",)),
        name="residual_layernorm",
    )(f, x, g.reshape(1, d), b.reshape(1, d))


def _top_values(s, k):
    vals = []
    for _ in range(k):
        m = jnp.max(s, axis=0, keepdims=True)
        vals.append(m)
        s = jnp.where(s == m, BELOW_NEG, s)
    return vals


def _peer_route(q_ref, sk_ref, s1_s, e1_s, c2_s, e2_s):
    k = PEER_TOPK
    for h in range(PEER_HEADS):
        qa = q_ref[:, (2 * h) * PEER_HALF:(2 * h + 1) * PEER_HALF]
        qb = q_ref[:, (2 * h + 1) * PEER_HALF:(2 * h + 2) * PEER_HALF]
        s1 = lax.dot_general(sk_ref[h, 0], qa, _NT, preferred_element_type=F32)
        s2 = lax.dot_general(sk_ref[h, 1], qb, _NT, preferred_element_type=F32)
        v1 = _top_values(s1, k)
        v2 = _top_values(s2, k)
        cand = [v1[a] + v2[b] for a in range(k) for b in range(k) if (a + 1) * (b + 1) <= k]
        pad = (-len(cand)) % V7X_SUBLANES
        cand = jnp.concatenate(cand + [jnp.full_like(cand[0], BELOW_NEG)] * pad, axis=0)
        best = _top_values(cand, k)
        thr = best[k - 1]
        z = sum(jnp.exp(bv - best[0]) for bv in best)
        c2 = jnp.full_like(s2, ABOVE_ALL)
        for a in range(k):
            c2 = jnp.where(v1[a] + s2 >= thr, v1[a], c2)
        s1_s[h] = s1
        e1_s[h] = jnp.exp(s1 - v1[0]) / z
        c2_s[h] = c2
        e2_s[h] = jnp.exp(s2 - v2[0])


def _gelu_exact(x):
    return 0.5 * x * (1.0 + lax.erf(x * math.sqrt(0.5)))


def _peer_mix(sc_ref, ht_ref, s1g, e1g, row_off, c2_s, e2_s, tc):
    n_exp = sc_ref.shape[0]
    toks = slice(tc * V7X_LANES, (tc + 1) * V7X_LANES)
    for ii in range(n_exp // PEER_N_KEYS):
        exps = slice(ii * PEER_N_KEYS, (ii + 1) * PEER_N_KEYS)
        g = row_off + ii
        w = jnp.zeros((PEER_N_KEYS, V7X_LANES), F32)
        for h in range(PEER_HEADS):
            hit = s1g[h][tc][g:g + 1, :] >= c2_s[h, :, toks]
            w = w + e1g[h][tc][g:g + 1, :] * jnp.where(hit, e2_s[h, :, toks], 0.0)
        ht_ref[toks, exps] = (w * _gelu_exact(sc_ref[exps, toks])).T


def _peer_kernel(q_ref, sk_ref, x_ref, u_ref, v_ref, o_ref, s1_s, e1_s, c2_s, e2_s, sc_s, ht_s):
    k = pl.program_id(1)
    te = u_ref.shape[0]
    tt = x_ref.shape[0]
    assert te // PEER_N_KEYS == V7X_SUBLANES

    @pl.when(k == 0)
    def _():
        _peer_route(q_ref, sk_ref, s1_s, e1_s, c2_s, e2_s)
        o_ref[...] = jnp.zeros_like(o_ref)

    i0 = pl.multiple_of(k * V7X_SUBLANES, V7X_SUBLANES)
    lanes = [slice(tc * V7X_LANES, (tc + 1) * V7X_LANES) for tc in range(tt // V7X_LANES)]
    s1g = [[s1_s[h, pl.ds(i0, V7X_SUBLANES), c] for c in lanes] for h in range(PEER_HEADS)]
    e1g = [[e1_s[h, pl.ds(i0, V7X_SUBLANES), c] for c in lanes] for h in range(PEER_HEADS)]

    pre_exp = 2 * V7X_LANES
    acc_exp = 4 * V7X_LANES
    for e0 in range(0, te, pre_exp):
        chunk = slice(e0, e0 + pre_exp)
        pre = lax.dot_general(x_ref[...], u_ref[chunk, :], _NT,
                              preferred_element_type=F32)
        sc_s[chunk, :] = pre.T
    for a0 in range(0, te, acc_exp):
        for e0 in range(a0, a0 + acc_exp, pre_exp):
            chunk = slice(e0, e0 + pre_exp)
            for tc in range(len(lanes)):
                _peer_mix(sc_s.at[chunk], ht_s.at[:, chunk], s1g, e1g, e0 // PEER_N_KEYS,
                          c2_s, e2_s, tc)
        done = slice(a0, a0 + acc_exp)
        o_ref[...] += lax.dot_general(ht_s[:, done], v_ref[done, :], (((1,), (0,)), ((), ())),
                                      preferred_element_type=F32)


def _peer(q, sub_keys, x, u, v, *, tt, te):
    t, d = x.shape
    n_exp = u.shape[0]
    tok = lambda shape: pltpu.VMEM(shape, F32)
    per_head = (PEER_HEADS, PEER_N_KEYS, tt)
    return pl.pallas_call(
        _peer_kernel,
        grid=(t // tt, n_exp // te),
        in_specs=[
            pl.BlockSpec((tt, q.shape[1]), lambda i, k: (i, 0)),
            pl.BlockSpec(sub_keys.shape, lambda i, k: (0, 0, 0, 0)),
            pl.BlockSpec((tt, d), lambda i, k: (i, 0)),
            pl.BlockSpec((te, d), lambda i, k: (k, 0)),
            pl.BlockSpec((te, d), lambda i, k: (k, 0)),
        ],
        out_specs=pl.BlockSpec((tt, d), lambda i, k: (i, 0)),
        out_shape=jax.ShapeDtypeStruct((t, d), F32),
        scratch_shapes=[tok(per_head), tok(per_head), tok(per_head), tok(per_head),
                        tok((te, tt)), tok((tt, te))],
        compiler_params=_cparams(("arbitrary", "arbitrary")),
        name="peer_experts",
    )(q, sub_keys, x, u, v)


def _tile(n, want):
    return want if n % want == 0 else n


def kernel(x, w_in, w_br_moba, w_br_diff, w_out, diff_lambda, diff_subln_g, ln1_g, ln1_b,
           peer_w_q, peer_sub_keys, peer_u, peer_v, ln2_g, ln2_b):
    batch, seq, d = x.shape
    depth = w_in.shape[0]
    t = batch * seq
    alpha = (2 * depth) ** 0.25
    moba_w = MOBA_HEADS * HEAD_DIM
    qkv_cols = 3 * moba_w + 3 * DIFF_HEADS * DIFF_V_DIM
    tables = _rope_tables(seq)
    tm_proj = _tile(seq, 1024)

    xf = x.reshape(t, d)
    xb = xf.astype(BF16)
    for l in range(depth):
        lambda_init = 0.8 - 0.6 * math.exp(-0.3 * l)
        w_in_b = w_in[l].astype(BF16)
        qkv = _project(xb, w_in_b, tables, mode="rope", col_off=0, n_cols=qkv_cols, seq=seq,
                       tm=tm_proj, tn=moba_w, out_dtype=BF16)
        gates = _project(xb, w_in_b, tables, mode="sigmoid", col_off=qkv_cols, n_cols=2 * d,
                         seq=seq, tm=tm_proj, tn=min(moba_w, 2 * d), out_dtype=BF16)
        y_m = _moba(qkv, batch=batch, seq=seq)
        y_d = _diff(qkv, diff_lambda[l], diff_subln_g[l], batch=batch, seq=seq,
                    lambda_init=lambda_init, tq=_tile(seq, 256))
        merged = _merge(y_m, y_d, w_br_moba[l].astype(BF16), w_br_diff[l].astype(BF16), gates,
                        tm=_tile(t, 1024), tn=_tile(d, 1024))
        xf, xb = _outproj_ln(merged, w_out[l].astype(BF16), xf, ln1_g[l], ln1_b[l],
                             alpha=alpha, tm=_tile(t, 256))
        q = _project(xb, peer_w_q[l].astype(BF16), tables, mode="plain", col_off=0,
                     n_cols=peer_w_q.shape[2], seq=seq, tm=tm_proj, tn=_tile(peer_w_q.shape[2], 1024),
                     out_dtype=BF16)
        ffn = _peer(q, peer_sub_keys[l].astype(BF16), xf, peer_u[l].astype(BF16),
                    peer_v[l].astype(BF16), tt=_tile(t, 512), te=1024)
        xf, xb = _resid_ln(ffn, xf, ln2_g[l], ln2_b[l], alpha=alpha, tm=_tile(t, 512))
    return xf.reshape(batch, seq, d)
```

```python
import functools
import math

import jax
import jax.numpy as jnp
from jax import lax
from jax.experimental import pallas as pl
from jax.experimental.pallas import tpu as pltpu

F32 = jnp.float32
BF16 = jnp.bfloat16

HEAD_DIM = 128
ROPE_DIM = HEAD_DIM // 4
ROPE_THETA = 500000.0

MOBA_HEADS = 8
MOBA_BLOCK = 256
MOBA_TOPK = 3

DIFF_HEADS = 4
DIFF_V_DIM = 2 * HEAD_DIM

PEER_HEADS = 8
PEER_N_KEYS = 128
PEER_HALF = 128
PEER_TOPK = 16

LN_EPS = 1e-5
RMS_EPS = 1e-5
NEG = -1e30
LOG2_E = math.log2(math.e)
BELOW_NEG = -3e38
ABOVE_ALL = 3e38

V7X_LANES = 128
V7X_SUBLANES = 8
V7X_VMEM_LIMIT = 56 * 1024 * 1024
CAST_TILE_BYTES = 8 * 1024 * 1024

_NT = (((1,), (1,)), ((), ()))


def _cparams(sem):
    return pltpu.CompilerParams(dimension_semantics=sem, vmem_limit_bytes=V7X_VMEM_LIMIT)


def _rope(a, c, s_up, s_dn):
    half = ROPE_DIM // 2
    return (a * c + pltpu.roll(a, half, 1) * s_up + pltpu.roll(a, V7X_LANES - half, 1) * s_dn)


def _proj_kernel(x_ref, w_ref, c_ref, su_ref, sd_ref, o_ref, *, mode):
    acc = jnp.dot(x_ref[...], w_ref[...], preferred_element_type=F32)
    if mode == "sigmoid":
        o_ref[...] = jax.nn.sigmoid(acc).astype(o_ref.dtype)
    elif mode == "plain":
        o_ref[...] = acc.astype(o_ref.dtype)
    else:
        j = pl.program_id(0)
        is_value = jnp.logical_or(j == 2, j == 5)

        @pl.when(is_value)
        def _():
            o_ref[...] = acc.astype(o_ref.dtype)

        @pl.when(jnp.logical_not(is_value))
        def _():
            c, su, sd = c_ref[...], su_ref[...], sd_ref[...]
            for h in range(acc.shape[1] // HEAD_DIM):
                sl = slice(h * HEAD_DIM, (h + 1) * HEAD_DIM)
                o_ref[:, sl] = _rope(acc[:, sl], c, su, sd).astype(o_ref.dtype)


def _project(x, w, tables, *, mode, col_off, n_cols, seq, tm, tn, out_dtype):
    t, k = x.shape
    off = col_off // tn
    pos_blocks = seq // tm
    tab_spec = pl.BlockSpec((tm, HEAD_DIM), lambda j, i: (i % pos_blocks, 0))
    return pl.pallas_call(
        functools.partial(_proj_kernel, mode=mode),
        grid=(n_cols // tn, t // tm),
        in_specs=[
            pl.BlockSpec((tm, k), lambda j, i: (i, 0)),
            pl.BlockSpec((k, tn), lambda j, i: (0, j + off)),
            tab_spec, tab_spec, tab_spec,
        ],
        out_specs=pl.BlockSpec((tm, tn), lambda j, i: (i, j)),
        out_shape=jax.ShapeDtypeStruct((t, n_cols), out_dtype),
        compiler_params=_cparams(("arbitrary", "arbitrary")),
        name=f"proj_{mode}",
    )(x, w, *tables)


def _rope_tables(seq):
    half = ROPE_DIM // 2
    inv_freq = 1.0 / (ROPE_THETA ** (jnp.arange(half, dtype=F32) / half))
    ang = jnp.arange(seq).astype(F32)[:, None] * inv_freq[None, :]
    cos, sin = jnp.cos(ang).astype(F32), jnp.sin(ang).astype(F32)
    ones = jnp.ones((seq, HEAD_DIM - ROPE_DIM), F32)
    zeros = jnp.zeros((seq, HEAD_DIM - ROPE_DIM), F32)
    zh = jnp.zeros((seq, half), F32)
    c = jnp.concatenate([cos, cos, ones], -1)
    s_up = jnp.concatenate([zh, sin, zeros], -1)
    s_dn = jnp.concatenate([-sin, zh, zeros], -1)
    return c, s_up, s_dn


def _lane_tile(a, width):
    return a if width == a.shape[1] else jnp.concatenate([a] * (width // a.shape[1]), axis=1)


def _softmax_block(s, vb, m_ref, l_ref, acc_ref, first):
    m_cur = jnp.max(s, axis=-1, keepdims=True)
    if first:
        m_new = jnp.broadcast_to(m_cur, m_ref.shape)
    else:
        m_prev = m_ref[...]
        m_new = jnp.maximum(m_prev, m_cur)
        alpha = jnp.exp2(m_prev - m_new)
    p = jnp.exp2(s - _lane_tile(m_new, s.shape[1]))
    l_cur = jnp.sum(p, axis=-1, keepdims=True)
    pv = lax.dot_general(p, vb, (((1,), (0,)), ((), ())), preferred_element_type=F32)
    if first:
        l_ref[...] = jnp.broadcast_to(l_cur, l_ref.shape)
        acc_ref[...] = pv
    else:
        l_ref[...] = alpha * l_ref[...] + l_cur
        acc_ref[...] = _lane_tile(alpha, pv.shape[1]) * acc_ref[...] + pv
    m_ref[...] = m_new


def _moba_kernel(q_ref, k_ref, v_ref, o_ref, kmean_s, sel_s, m_s, l_s, acc_s, *, n_sel, scale):
    blk = q_ref.shape[0]
    seq = k_ref.shape[0]
    heads = q_ref.shape[1] // HEAD_DIM
    own = pl.program_id(1)
    head_cols = [slice(h * HEAD_DIM, (h + 1) * HEAD_DIM) for h in range(heads)]

    @pl.when(own == 0)
    def _():
        row = lax.broadcasted_iota(jnp.int32, (V7X_LANES, seq), 0)
        col = lax.broadcasted_iota(jnp.int32, (V7X_LANES, seq), 1)
        inside = jnp.logical_and(col >= row * blk, col < (row + 1) * blk)
        avg = jnp.where(inside, 1.0 / blk, 0.0).astype(BF16)
        kmean_s[...] = jnp.dot(avg, k_ref[...], preferred_element_type=F32)

    lane = lax.broadcasted_iota(jnp.int32, (blk, V7X_LANES), 1)
    lane_f = lane.astype(F32)
    past = lane < own
    for h, hc in enumerate(head_cols):
        gate = lax.dot_general(q_ref[:, hc], kmean_s[:, hc].astype(BF16), _NT,
                               preferred_element_type=F32)
        g = jnp.where(past, gate, NEG)
        sel = jnp.zeros(gate.shape, F32)
        for _ in range(n_sel):
            m = jnp.max(g, axis=-1, keepdims=True)
            idx = jnp.min(jnp.where(g == m, lane_f, float(V7X_LANES)), axis=-1, keepdims=True)
            hit = lane_f == idx
            sel = jnp.where(hit, 1.0, sel)
            g = jnp.where(hit, BELOW_NEG, g)
        sel_s[h] = jnp.where(past, sel, 0.0)

    def scores(h, start):
        kb = k_ref[pl.ds(start, blk), head_cols[h]]
        return lax.dot_general(q_ref[:, head_cols[h]], kb, _NT,
                               preferred_element_type=F32) * (scale * LOG2_E)

    start0 = pl.multiple_of(own * blk, blk)
    r = lax.broadcasted_iota(jnp.int32, (blk, blk), 0)
    c = lax.broadcasted_iota(jnp.int32, (blk, blk), 1)
    causal = c <= r
    for h, hc in enumerate(head_cols):
        s = jnp.where(causal, scores(h, start0), NEG)
        _softmax_block(s, v_ref[pl.ds(start0, blk), hc], m_s.at[h], l_s.at[h], acc_s.at[h], True)

    def body(n, carry):
        start = pl.multiple_of(n * blk, blk)
        for h, hc in enumerate(head_cols):
            chosen = jnp.sum(jnp.where(lane == n, sel_s[h], 0.0), axis=-1, keepdims=True) > 0.0
            s = jnp.where(chosen, scores(h, start), NEG)
            _softmax_block(s, v_ref[pl.ds(start, blk), hc], m_s.at[h], l_s.at[h], acc_s.at[h], False)
        return carry

    lax.fori_loop(0, own, body, 0)
    for h, hc in enumerate(head_cols):
        o_ref[:, hc] = (acc_s[h] / l_s[h]).astype(o_ref.dtype)


def _moba(qkv, *, batch, seq):
    t = qkv.shape[0]
    blk = MOBA_BLOCK
    nq = seq // blk
    n_sel = max(1, min(MOBA_TOPK, nq - 1))
    h = MOBA_HEADS
    width = h * HEAD_DIM
    stat = pltpu.VMEM((h, blk, V7X_LANES), F32)
    return pl.pallas_call(
        functools.partial(_moba_kernel, n_sel=n_sel, scale=HEAD_DIM ** -0.5),
        grid=(batch, nq),
        in_specs=[
            pl.BlockSpec((blk, width), lambda b, i: (b * nq + i, 0)),
            pl.BlockSpec((seq, width), lambda b, i: (b, 1)),
            pl.BlockSpec((seq, width), lambda b, i: (b, 2)),
        ],
        out_specs=pl.BlockSpec((blk, width), lambda b, i: (b * nq + i, 0)),
        out_shape=jax.ShapeDtypeStruct((t, width), BF16),
        scratch_shapes=[pltpu.VMEM((V7X_LANES, width), F32), stat, stat, stat,
                        pltpu.VMEM((h, blk, HEAD_DIM), F32)],
        compiler_params=_cparams(("arbitrary", "arbitrary")),
        name="moba_attention",
    )(qkv, qkv, qkv)


def _diff_kernel(lam_ref, g_ref, q_ref, k_ref, v_ref, o_ref, m_s, l_s, acc_s, *, scale, lambda_init):
    tq = q_ref.shape[0]
    heads = q_ref.shape[1] // (2 * HEAD_DIM)
    qi = pl.program_id(1)
    lv = lam_ref[...]
    lam = (jnp.exp(jnp.sum(lv[0:1] * lv[1:2], axis=-1, keepdims=True))
           - jnp.exp(jnp.sum(lv[2:3] * lv[3:4], axis=-1, keepdims=True)) + lambda_init)

    def qk_cols(h, c):
        return slice((2 * h + c) * HEAD_DIM, (2 * h + c + 1) * HEAD_DIM)

    def v_cols(h):
        return slice(h * DIFF_V_DIM, (h + 1) * DIFF_V_DIM)

    def step(start, mask, first):
        for h in range(heads):
            vb = v_ref[pl.ds(start, tq), v_cols(h)]
            for c in range(2):
                s = lax.dot_general(q_ref[:, qk_cols(h, c)], k_ref[pl.ds(start, tq), qk_cols(h, c)],
                                    _NT, preferred_element_type=F32) * (scale * LOG2_E)
                if mask is not None:
                    s = jnp.where(mask, s, NEG)
                i = 2 * h + c
                _softmax_block(s, vb, m_s.at[i], l_s.at[i], acc_s.at[i], first)

    r = lax.broadcasted_iota(jnp.int32, (tq, tq), 0)
    c = lax.broadcasted_iota(jnp.int32, (tq, tq), 1)
    step(pl.multiple_of(qi * tq, tq), c <= r, True)

    def body(n, carry):
        step(pl.multiple_of(n * tq, tq), None, False)
        return carry

    lax.fori_loop(0, qi, body, 0)
    for h in range(heads):
        a1 = acc_s[2 * h] / _lane_tile(l_s[2 * h], DIFF_V_DIM)
        a2 = acc_s[2 * h + 1] / _lane_tile(l_s[2 * h + 1], DIFF_V_DIM)
        o = a1 - lam * a2
        o = o * lax.rsqrt(jnp.mean(jnp.square(o), axis=-1, keepdims=True) + RMS_EPS) * g_ref[...]
        o_ref[:, v_cols(h)] = (o * (1.0 - lambda_init)).astype(o_ref.dtype)


def _diff(qkv, diff_lambda, subln_g, *, batch, seq, lambda_init, tq):
    t = qkv.shape[0]
    nq = seq // tq
    h = DIFF_HEADS
    width = h * DIFF_V_DIM
    base = 3 * MOBA_HEADS * HEAD_DIM // width
    stat = pltpu.VMEM((2 * h, tq, V7X_LANES), F32)
    return pl.pallas_call(
        functools.partial(_diff_kernel, scale=HEAD_DIM ** -0.5, lambda_init=lambda_init),
        grid=(batch, nq),
        in_specs=[
            pl.BlockSpec((4, HEAD_DIM), lambda b, i: (0, 0)),
            pl.BlockSpec((1, DIFF_V_DIM), lambda b, i: (0, 0)),
            pl.BlockSpec((tq, width), lambda b, i: (b * nq + i, base)),
            pl.BlockSpec((seq, width), lambda b, i: (b, base + 1)),
            pl.BlockSpec((seq, width), lambda b, i: (b, base + 2)),
        ],
        out_specs=pl.BlockSpec((tq, width), lambda b, i: (b * nq + i, 0)),
        out_shape=jax.ShapeDtypeStruct((t, width), BF16),
        scratch_shapes=[stat, stat, pltpu.VMEM((2 * h, tq, DIFF_V_DIM), F32)],
        compiler_params=_cparams(("arbitrary", "arbitrary")),
        name="diff_attention",
    )(diff_lambda, subln_g.reshape(1, DIFF_V_DIM), qkv, qkv, qkv)


def _merge_kernel(ym_ref, yd_ref, wm_ref, wd_ref, gm_ref, gd_ref, o_ref):
    bm = jnp.dot(ym_ref[...], wm_ref[...], preferred_element_type=F32)
    bd = jnp.dot(yd_ref[...], wd_ref[...], preferred_element_type=F32)
    o_ref[...] = (gm_ref[...].astype(F32) * bm + gd_ref[...].astype(F32) * bd).astype(o_ref.dtype)


def _merge(y_m, y_d, w_m, w_d, gates, *, tm, tn):
    t, km = y_m.shape
    kd = y_d.shape[1]
    d = w_m.shape[1]
    nj = d // tn
    return pl.pallas_call(
        _merge_kernel,
        grid=(t // tm, nj),
        in_specs=[
            pl.BlockSpec((tm, km), lambda i, j: (i, 0)),
            pl.BlockSpec((tm, kd), lambda i, j: (i, 0)),
            pl.BlockSpec((km, tn), lambda i, j: (0, j)),
            pl.BlockSpec((kd, tn), lambda i, j: (0, j)),
            pl.BlockSpec((tm, tn), lambda i, j: (i, j)),
            pl.BlockSpec((tm, tn), lambda i, j: (i, nj + j)),
        ],
        out_specs=pl.BlockSpec((tm, tn), lambda i, j: (i, j)),
        out_shape=jax.ShapeDtypeStruct((t, d), BF16),
        compiler_params=_cparams(("arbitrary", "arbitrary")),
        name="branch_merge",
    )(y_m, y_d, w_m, w_d, gates, gates)


def _layer_norm_rows(y, g, b):
    mu = jnp.mean(y, axis=-1, keepdims=True)
    var = jnp.mean(jnp.square(y - mu), axis=-1, keepdims=True)
    return (y - mu) * lax.rsqrt(var + LN_EPS) * g + b


def _outproj_ln_kernel(m_ref, w_ref, x_ref, g_ref, b_ref, of_ref, ob_ref, *, alpha):
    mix = jnp.dot(m_ref[...], w_ref[...], preferred_element_type=F32)
    y = _layer_norm_rows(alpha * x_ref[...] + mix, g_ref[...], b_ref[...])
    of_ref[...] = y
    ob_ref[...] = y.astype(BF16)


def _outproj_ln(merged, w_out, x, g, b, *, alpha, tm):
    t, d = x.shape
    row = pl.BlockSpec((tm, d), lambda i: (i, 0))
    vec = pl.BlockSpec((1, d), lambda i: (0, 0))
    return pl.pallas_call(
        functools.partial(_outproj_ln_kernel, alpha=alpha),
        grid=(t // tm,),
        in_specs=[row, pl.BlockSpec((d, d), lambda i: (0, 0)), row, vec, vec],
        out_specs=[row, row],
        out_shape=[jax.ShapeDtypeStruct((t, d), F32), jax.ShapeDtypeStruct((t, d), BF16)],
        compiler_params=_cparams(("arbitrary",)),
        name="outproj_layernorm",
    )(merged, w_out, x, g.reshape(1, d), b.reshape(1, d))


def _resid_ln_kernel(f_ref, x_ref, g_ref, b_ref, of_ref, ob_ref, *, alpha):
    y = _layer_norm_rows(alpha * x_ref[...] + f_ref[...], g_ref[...], b_ref[...])
    of_ref[...] = y
    ob_ref[...] = y.astype(BF16)


def _resid_ln(f, x, g, b, *, alpha, tm):
    t, d = x.shape
    row = pl.BlockSpec((tm, d), lambda i: (i, 0))
    vec = pl.BlockSpec((1, d), lambda i: (0, 0))
    return pl.pallas_call(
        functools.partial(_resid_ln_kernel, alpha=alpha),
        grid=(t // tm,),
        in_specs=[row, row, vec, vec],
        out_specs=[row, row],
        out_shape=[jax.ShapeDtypeStruct((t, d), F32), jax.ShapeDtypeStruct((t, d), BF16)],
        compiler_params=_cparams(("arbitrary",)),
        name="residual_layernorm",
    )(f, x, g.reshape(1, d), b.reshape(1, d))


def _top_values(s, k):
    vals = []
    for _ in range(k):
        m = jnp.max(s, axis=0, keepdims=True)
        vals.append(m)
        s = jnp.where(s == m, BELOW_NEG, s)
    return vals


def _peer_route(q_ref, sk_ref, s1_s, e1_s, c2_s, e2_s):
    k = PEER_TOPK
    for h in range(PEER_HEADS):
        qa = q_ref[:, (2 * h) * PEER_HALF:(2 * h + 1) * PEER_HALF]
        qb = q_ref[:, (2 * h + 1) * PEER_HALF:(2 * h + 2) * PEER_HALF]
        s1 = lax.dot_general(sk_ref[h, 0], qa, _NT, preferred_element_type=F32)
        s2 = lax.dot_general(sk_ref[h, 1], qb, _NT, preferred_element_type=F32)
        v1 = _top_values(s1, k)
        v2 = _top_values(s2, k)
        cand = [v1[a] + v2[b] for a in range(k) for b in range(k) if (a + 1) * (b + 1) <= k]
        pad = (-len(cand)) % V7X_SUBLANES
        cand = jnp.concatenate(cand + [jnp.full_like(cand[0], BELOW_NEG)] * pad, axis=0)
        best = _top_values(cand, k)
        thr = best[k - 1]
        z = sum(jnp.exp(bv - best[0]) for bv in best)
        c2 = jnp.full_like(s2, ABOVE_ALL)
        for a in range(k):
            c2 = jnp.where(v1[a] + s2 >= thr, v1[a], c2)
        s1_s[h] = s1
        e1_s[h] = jnp.exp(s1 - v1[0]) * (0.5 / z)
        c2_s[h] = c2
        e2_s[h] = jnp.exp(s2 - v2[0])


def _gelu_doubled(x):
    return x * (1.0 + lax.erf(x * math.sqrt(0.5)))


def _peer_mix(sc_ref, ht_ref, s1g, e1g, row_off, c2_s, e2_s, tc):
    n_exp = sc_ref.shape[0]
    toks = slice(tc * V7X_LANES, (tc + 1) * V7X_LANES)
    for ii in range(n_exp // PEER_N_KEYS):
        exps = slice(ii * PEER_N_KEYS, (ii + 1) * PEER_N_KEYS)
        g = row_off + ii
        w = jnp.zeros((PEER_N_KEYS, V7X_LANES), F32)
        for h in range(PEER_HEADS):
            hit = s1g[h][tc][g:g + 1, :] >= c2_s[h, :, toks]
            w = w + e1g[h][tc][g:g + 1, :] * jnp.where(hit, e2_s[h, :, toks], 0.0)
        ht_ref[toks, exps] = (w * _gelu_doubled(sc_ref[exps, toks])).T


def _peer_kernel(q_ref, sk_ref, xt_ref, u_ref, v_ref, o_ref, s1_s, e1_s, c2_s, e2_s, sc_s, ht_s):
    k = pl.program_id(1)
    te = u_ref.shape[0]
    tt = xt_ref.shape[1]
    assert te // PEER_N_KEYS == V7X_SUBLANES

    @pl.when(k == 0)
    def _():
        _peer_route(q_ref, sk_ref, s1_s, e1_s, c2_s, e2_s)
        o_ref[...] = jnp.zeros_like(o_ref)

    i0 = pl.multiple_of(k * V7X_SUBLANES, V7X_SUBLANES)
    lanes = [slice(tc * V7X_LANES, (tc + 1) * V7X_LANES) for tc in range(tt // V7X_LANES)]
    s1g = [[s1_s[h, pl.ds(i0, V7X_SUBLANES), c] for c in lanes] for h in range(PEER_HEADS)]
    e1g = [[e1_s[h, pl.ds(i0, V7X_SUBLANES), c] for c in lanes] for h in range(PEER_HEADS)]

    sc_s[...] = jnp.dot(u_ref[...], xt_ref[...], preferred_element_type=F32)
    for tc in range(len(lanes)):
        _peer_mix(sc_s, ht_s, s1g, e1g, 0, c2_s, e2_s, tc)
    o_ref[...] += lax.dot_general(ht_s[...], v_ref[...], (((1,), (0,)), ((), ())),
                                  preferred_element_type=F32)


def _peer(q, sub_keys, x_t, u, v, *, tt, te):
    d, t = x_t.shape
    n_exp = u.shape[0]
    tok = lambda shape: pltpu.VMEM(shape, F32)
    per_head = (PEER_HEADS, PEER_N_KEYS, tt)
    return pl.pallas_call(
        _peer_kernel,
        grid=(t // tt, n_exp // te),
        in_specs=[
            pl.BlockSpec((tt, q.shape[1]), lambda i, k: (i, 0)),
            pl.BlockSpec(sub_keys.shape, lambda i, k: (0, 0, 0, 0)),
            pl.BlockSpec((d, tt), lambda i, k: (0, i)),
            pl.BlockSpec((te, d), lambda i, k: (k, 0)),
            pl.BlockSpec((te, d), lambda i, k: (k, 0)),
        ],
        out_specs=pl.BlockSpec((tt, d), lambda i, k: (i, 0)),
        out_shape=jax.ShapeDtypeStruct((t, d), F32),
        scratch_shapes=[tok(per_head), tok(per_head), tok(per_head), tok(per_head),
                        tok((te, tt)), tok((tt, te))],
        compiler_params=_cparams(("arbitrary", "arbitrary")),
        name="peer_experts",
    )(q, sub_keys, x_t, u, v)


def _cast_kernel(w_ref, o_ref):
    o_ref[...] = w_ref[...].astype(o_ref.dtype)


def _layer_bf16(w, l):
    _, rows, cols = w.shape
    tr = rows
    while tr * cols * 4 > CAST_TILE_BYTES and tr % 2 == 0 and (tr // 2) % 16 == 0:
        tr //= 2
    return pl.pallas_call(
        _cast_kernel,
        grid=(rows // tr,),
        in_specs=[pl.BlockSpec((None, tr, cols), lambda i: (l, i, 0))],
        out_specs=pl.BlockSpec((tr, cols), lambda i: (i, 0)),
        out_shape=jax.ShapeDtypeStruct((rows, cols), BF16),
        compiler_params=_cparams(("arbitrary",)),
        name="weight_cast",
    )(w)


def _tile(n, want):
    return want if n % want == 0 else n


def kernel(x, w_in, w_br_moba, w_br_diff, w_out, diff_lambda, diff_subln_g, ln1_g, ln1_b,
           peer_w_q, peer_sub_keys, peer_u, peer_v, ln2_g, ln2_b):
    batch, seq, d = x.shape
    depth = w_in.shape[0]
    t = batch * seq
    alpha = (2 * depth) ** 0.25
    moba_w = MOBA_HEADS * HEAD_DIM
    qkv_cols = 3 * moba_w + 3 * DIFF_HEADS * DIFF_V_DIM
    tables = _rope_tables(seq)
    tm_proj = _tile(seq, 1024)

    xf = x.reshape(t, d)
    xb = xf.astype(BF16)
    for l in range(depth):
        lambda_init = 0.8 - 0.6 * math.exp(-0.3 * l)
        w_in_b = _layer_bf16(w_in, l)
        qkv = _project(xb, w_in_b, tables, mode="rope", col_off=0, n_cols=qkv_cols, seq=seq,
                       tm=tm_proj, tn=moba_w, out_dtype=BF16)
        gates = _project(xb, w_in_b, tables, mode="sigmoid", col_off=qkv_cols, n_cols=2 * d,
                         seq=seq, tm=tm_proj, tn=min(moba_w, 2 * d), out_dtype=BF16)
        y_m = _moba(qkv, batch=batch, seq=seq)
        y_d = _diff(qkv, diff_lambda[l], diff_subln_g[l], batch=batch, seq=seq,
                    lambda_init=lambda_init, tq=_tile(seq, 256))
        merged = _merge(y_m, y_d, _layer_bf16(w_br_moba, l), _layer_bf16(w_br_diff, l), gates,
                        tm=_tile(t, 1024), tn=_tile(d, 1024))
        xf, xb = _outproj_ln(merged, _layer_bf16(w_out, l), xf, ln1_g[l], ln1_b[l],
                             alpha=alpha, tm=_tile(t, 512))
        q = _project(xb, _layer_bf16(peer_w_q, l), tables, mode="plain", col_off=0,
                     n_cols=peer_w_q.shape[2], seq=seq, tm=tm_proj, tn=_tile(peer_w_q.shape[2], 1024),
                     out_dtype=BF16)
        ffn = _peer(q, peer_sub_keys[l].astype(BF16), xb.T, _layer_bf16(peer_u, l),
                    _layer_bf16(peer_v, l), tt=_tile(t, 512), te=1024)
        xf, xb = _resid_ln(ffn, xf, ln2_g[l], ln2_b[l], alpha=alpha, tm=_tile(t, 512))
    return xf.reshape(batch, seq, d)
```
